```python
import jax, jax.numpy as jnp
from jax import lax
import numpy as np

D_MODEL = 1024
BATCH = 4
SEQ = 8192
DEPTH = 2

CHUNK = 64
LRU_WIDTH = D_MODEL // 2
LRU_BLOCKS = 8
LRU_BLOCK_DIM = LRU_WIDTH // LRU_BLOCKS
CONV_WIDTH = 4
LRU_C = 8.0
SB_HEADS = 8
SB_HEAD_DIM = (D_MODEL - LRU_WIDTH) // SB_HEADS
SB_WIDTH = SB_HEADS * SB_HEAD_DIM
MIX_WIDTH = LRU_WIDTH + SB_WIDTH
IN_WIDTH = 2 * LRU_WIDTH + 3 * SB_WIDTH
Q_BLOCK = 128
D_FF = 3 * D_MODEL
N_EXPERTS = 8
TOP_K = 2
EXPERT_BLOCK = 256
N_DENSE = (DEPTH + 1) // 2
N_MOE = DEPTH // 2
NORM_EPS = 1e-6

kernel_name = 'hymba_rglru_stickbreaking_moe_trunk'


def rms_norm(x, g):
    xf = x.astype(jnp.float32)
    y = xf * lax.rsqrt(jnp.mean(xf * xf, axis=-1, keepdims=True) + NORM_EPS)
    return (y * g.astype(jnp.float32)).astype(x.dtype)


def causal_depthwise_conv(x, w, b):
    s = x.shape[1]
    xp = jnp.pad(x, ((0, 0), (CONV_WIDTH - 1, 0), (0, 0)))
    return b + sum(w[j] * xp[:, j:j + s] for j in range(CONV_WIDTH))


def rg_lru(x, w_r, b_r, w_i, b_i, lam):
    bsz, s, _ = x.shape
    xb = x.reshape(bsz, s, LRU_BLOCKS, LRU_BLOCK_DIM)
    r = jax.nn.sigmoid((jnp.einsum('bsgi,gij->bsgj', xb, w_r) + b_r).astype(jnp.float32)).reshape(bsz, s, LRU_WIDTH)
    i = jax.nn.sigmoid((jnp.einsum('bsgi,gij->bsgj', xb, w_i) + b_i).astype(jnp.float32)).reshape(bsz, s, LRU_WIDTH)
    log_a = -LRU_C * r * jax.nn.softplus(-lam.astype(jnp.float32))
    a = jnp.exp(log_a)
    u = jnp.sqrt(-jnp.expm1(2.0 * log_a)) * (i * x.astype(jnp.float32))

    def combine(left, right):
        a_l, u_l = left
        a_r, u_r = right
        return a_l * a_r, a_r * u_l + u_r

    _, h = lax.associative_scan(combine, (a, u), axis=1)
    return h.astype(x.dtype)


def stick_breaking_attention(q, k, v):
    bsz, s, h, dh = q.shape
    n_blocks = s // Q_BLOCK
    scale = dh ** -0.5
    kf = k.astype(jnp.float32)
    vf = v.astype(jnp.float32)
    qb = q.astype(jnp.float32).reshape(bsz, n_blocks, Q_BLOCK, h, dh).transpose(1, 0, 3, 2, 4)
    key_pos = jnp.arange(s)

    def block(args):
        q_blk, blk_idx = args
        z = jnp.einsum('bhqd,bkhd->bhqk', q_blk, kf) * scale
        q_pos = blk_idx * Q_BLOCK + jnp.arange(Q_BLOCK)
        strict = key_pos[None, :] < q_pos[:, None]
        log_1m = jnp.where(strict, jax.nn.log_sigmoid(-z), 0.0)
        rev = lax.cumsum(log_1m, axis=log_1m.ndim - 1, reverse=True)
        later = jnp.concatenate([rev[..., 1:], jnp.zeros_like(rev[..., :1])], axis=-1)
        w = jnp.where(strict, jnp.exp(jax.nn.log_sigmoid(z) + later), 0.0)
        return jnp.einsum('bhqk,bkhd->bqhd', w, vf)

    out = lax.map(block, (qb, jnp.arange(n_blocks)))
    return out.transpose(1, 0, 2, 3, 4).reshape(bsz, s, h * dh).astype(q.dtype)


def hybrid_mixer(h, w_in, conv_w, conv_b, w_r, b_r, w_i, b_i, lam, g_lru, g_sb, w_out):
    bsz, s, _ = h.shape
    proj = h @ w_in
    splits = [LRU_WIDTH, 2 * LRU_WIDTH, 2 * LRU_WIDTH + SB_WIDTH, 2 * LRU_WIDTH + 2 * SB_WIDTH]
    x_lru, gate, q, k, v = jnp.split(proj, splits, axis=-1)
    x_lru = causal_depthwise_conv(x_lru, conv_w, conv_b)
    y_lru = rg_lru(x_lru, w_r, b_r, w_i, b_i, lam) * jax.nn.gelu(gate)
    shp = (bsz, s, SB_HEADS, SB_HEAD_DIM)
    y_sb = stick_breaking_attention(q.reshape(shp), k.reshape(shp), v.reshape(shp))
    y = jnp.concatenate([rms_norm(y_lru, g_lru), rms_norm(y_sb, g_sb)], axis=-1)
    return y @ w_out


def swiglu(h, w_g, w_u, w_d):
    return (jax.nn.silu(h @ w_g) * (h @ w_u)) @ w_d


def moe_swiglu(h, w_router, w_g, w_u, w_d):
    bsz, s, d = h.shape
    hf = h.reshape(-1, d)
    n_tok = hf.shape[0]
    n_assign = n_tok * TOP_K
    logits = (hf @ w_router).astype(jnp.float32)
    top_logit, top_idx = lax.top_k(logits, TOP_K)
    gates = jax.nn.softmax(top_logit, axis=-1)
    flat_e = top_idx.reshape(-1)
    flat_tok = jnp.repeat(jnp.arange(n_tok, dtype=jnp.int32), TOP_K)
    flat_gate = gates.reshape(-1)
    order = jnp.argsort(flat_e)
    e_sorted = flat_e[order]
    counts = jnp.bincount(flat_e, length=N_EXPERTS)
    padded = (counts + EXPERT_BLOCK - 1) // EXPERT_BLOCK * EXPERT_BLOCK
    pad_end = jnp.cumsum(padded)
    pad_start = pad_end - padded
    start = jnp.cumsum(counts) - counts
    dest = pad_start[e_sorted] + jnp.arange(n_assign) - start[e_sorted]
    n_blocks = -(-n_assign // EXPERT_BLOCK) + N_EXPERTS
    n_rows = n_blocks * EXPERT_BLOCK
    row_tok = jnp.zeros((n_rows,), jnp.int32).at[dest].set(flat_tok[order])
    row_gate = jnp.zeros((n_rows,), jnp.float32).at[dest].set(flat_gate[order])
    block_start = jnp.arange(n_blocks) * EXPERT_BLOCK
    block_expert = jnp.minimum(jnp.searchsorted(pad_end, block_start, side='right'), N_EXPERTS - 1)
    x_rows = hf[row_tok].reshape(n_blocks, EXPERT_BLOCK, d)

    def expert_block(args):
        xb, e = args
        return swiglu(xb, w_g[e], w_u[e], w_d[e])

    y_rows = lax.map(expert_block, (x_rows, block_expert)).reshape(n_rows, d)
    y = jnp.zeros_like(hf).at[row_tok].add(y_rows * row_gate[:, None].astype(hf.dtype))
    return y.reshape(bsz, s, d)


def setup_inputs(seed: int = 0) -> dict:
    key = jax.random.key(seed)
    ks = jax.random.split(key, 24)
    f32 = jnp.float32
    nrm = lambda k, shp, sc: jax.random.normal(k, shp, f32) * sc
    a0 = jax.random.uniform(ks[9], (DEPTH, LRU_WIDTH), f32, 0.9, 0.999)
    sig = a0 ** (1.0 / LRU_C)
    lru_lambda = jnp.log(sig) - jnp.log1p(-sig)
    return {
        'x': nrm(ks[0], (BATCH, SEQ, D_MODEL), 1.0),
        'mix_norm': 1.0 + nrm(ks[1], (DEPTH, D_MODEL), 0.01),
        'w_in': nrm(ks[2], (DEPTH, D_MODEL, IN_WIDTH), D_MODEL ** -0.5),
        'conv_w': nrm(ks[3], (DEPTH, CONV_WIDTH, LRU_WIDTH), CONV_WIDTH ** -0.5),
        'conv_b': nrm(ks[4], (DEPTH, LRU_WIDTH), 0.01),
        'w_rgate': nrm(ks[5], (DEPTH, LRU_BLOCKS, LRU_BLOCK_DIM, LRU_BLOCK_DIM), LRU_BLOCK_DIM ** -0.5),
        'b_rgate': nrm(ks[6], (DEPTH, LRU_BLOCKS, LRU_BLOCK_DIM), 0.01),
        'w_igate': nrm(ks[7], (DEPTH, LRU_BLOCKS, LRU_BLOCK_DIM, LRU_BLOCK_DIM), LRU_BLOCK_DIM ** -0.5),
        'b_igate': nrm(ks[8], (DEPTH, LRU_BLOCKS, LRU_BLOCK_DIM), 0.01),
        'lru_lambda': lru_lambda,
        'lru_out_norm': 1.0 + nrm(ks[10], (DEPTH, LRU_WIDTH), 0.01),
        'sb_out_norm': 1.0 + nrm(ks[11], (DEPTH, SB_WIDTH), 0.01),
        'w_out': nrm(ks[12], (DEPTH, MIX_WIDTH, D_MODEL), MIX_WIDTH ** -0.5),
        'ffn_norm': 1.0 + nrm(ks[13], (DEPTH, D_MODEL), 0.01),
        'dense_w_gate': nrm(ks[14], (N_DENSE, D_MODEL, D_FF), D_MODEL ** -0.5),
        'dense_w_up': nrm(ks[15], (N_DENSE, D_MODEL, D_FF), D_MODEL ** -0.5),
        'dense_w_down': nrm(ks[16], (N_DENSE, D_FF, D_MODEL), D_FF ** -0.5),
        'router_w': nrm(ks[17], (N_MOE, D_MODEL, N_EXPERTS), D_MODEL ** -0.5),
        'moe_w_gate': nrm(ks[18], (N_MOE, N_EXPERTS, D_MODEL, D_FF), D_MODEL ** -0.5),
        'moe_w_up': nrm(ks[19], (N_MOE, N_EXPERTS, D_MODEL, D_FF), D_MODEL ** -0.5),
        'moe_w_down': nrm(ks[20], (N_MOE, N_EXPERTS, D_FF, D_MODEL), D_FF ** -0.5),
        'final_norm': 1.0 + nrm(ks[21], (D_MODEL,), 0.01),
    }


def reference(x, mix_norm, w_in, conv_w, conv_b, w_rgate, b_rgate, w_igate, b_igate, lru_lambda,
              lru_out_norm, sb_out_norm, w_out, ffn_norm, dense_w_gate, dense_w_up, dense_w_down,
              router_w, moe_w_gate, moe_w_up, moe_w_down, final_norm):
    for layer in range(DEPTH):
        h = rms_norm(x, mix_norm[layer])
        x = x + hybrid_mixer(h, w_in[layer], conv_w[layer], conv_b[layer], w_rgate[layer], b_rgate[layer],
                             w_igate[layer], b_igate[layer], lru_lambda[layer], lru_out_norm[layer],
                             sb_out_norm[layer], w_out[layer])
        h = rms_norm(x, ffn_norm[layer])
        j = layer // 2
        if layer % 2 == 0:
            x = x + swiglu(h, dense_w_gate[j], dense_w_up[j], dense_w_down[j])
        else:
            x = x + moe_swiglu(h, router_w[j], moe_w_gate[j], moe_w_up[j], moe_w_down[j])
    return rms_norm(x, final_norm)
```

```python
import functools

import jax
import jax.numpy as jnp
from jax import lax
from jax.experimental import pallas as pl
from jax.experimental.pallas import tpu as pltpu

D_MODEL = 1024
LRU_WIDTH = 512
LRU_BLOCKS = 8
LRU_BLOCK_DIM = LRU_WIDTH // LRU_BLOCKS
CONV_WIDTH = 4
LRU_C = 8.0
SB_HEADS = 8
SB_HEAD_DIM = 64
SB_WIDTH = SB_HEADS * SB_HEAD_DIM
D_FF = 3072
N_EXPERTS = 8
TOP_K = 2
NORM_EPS = 1e-6

LANES = 128
VMEM_LIMIT = 56 * 1024 * 1024

BF16 = jnp.bfloat16
F32 = jnp.float32


def _params(*sem):
    return pltpu.CompilerParams(dimension_semantics=sem, vmem_limit_bytes=VMEM_LIMIT)


def _rms(x, g):
    return x * lax.rsqrt(jnp.mean(x * x, axis=-1, keepdims=True) + NORM_EPS) * g


def _sigmoid(x):
    return 1.0 / (1.0 + jnp.exp(-x))


def _softplus(x):
    return jnp.maximum(x, 0.0) + jnp.log(1.0 + jnp.exp(-jnp.abs(x)))


def _norm_inproj_kernel(x_ref, g_ref, wa_ref, wb_ref, xg_ref, qkv_ref):
    h = _rms(x_ref[...], g_ref[...]).astype(BF16)
    xg_ref[...] = jnp.dot(h, wa_ref[...], preferred_element_type=F32)
    p = jnp.dot(h, wb_ref[...], preferred_element_type=F32)
    scale = SB_HEAD_DIM ** -0.5
    qkv_ref[:, :SB_WIDTH] = (p[:, :SB_WIDTH] * scale).astype(BF16)
    qkv_ref[:, SB_WIDTH:] = p[:, SB_WIDTH:].astype(BF16)


def _norm_inproj(x, g, w_in, tm=512):
    n = x.shape[0]
    wa = w_in[:, :2 * LRU_WIDTH].astype(BF16)
    wb = w_in[:, 2 * LRU_WIDTH:].astype(BF16)
    return pl.pallas_call(
        _norm_inproj_kernel,
        grid=(n // tm,),
        in_specs=[
            pl.BlockSpec((tm, D_MODEL), lambda i: (i, 0)),
            pl.BlockSpec((1, D_MODEL), lambda i: (0, 0)),
            pl.BlockSpec((D_MODEL, 2 * LRU_WIDTH), lambda i: (0, 0)),
            pl.BlockSpec((D_MODEL, 3 * SB_WIDTH), lambda i: (0, 0)),
        ],
        out_specs=[
            pl.BlockSpec((tm, 2 * LRU_WIDTH), lambda i: (i, 0)),
            pl.BlockSpec((tm, 3 * SB_WIDTH), lambda i: (i, 0)),
        ],
        out_shape=[
            jax.ShapeDtypeStruct((n, 2 * LRU_WIDTH), F32),
            jax.ShapeDtypeStruct((n, 3 * SB_WIDTH), BF16),
        ],
        compiler_params=_params("parallel"),
        name="norm_inproj",
    )(x, g.reshape(1, D_MODEL), wa, wb)


def _shift_rows(x, k, fill_first8):
    xs = pltpu.roll(x, k, axis=0)
    row8 = lax.broadcasted_iota(jnp.int32, (8, x.shape[1]), 0)
    first = jnp.where(row8 < k, fill_first8, xs[:8])
    return jnp.concatenate([first, xs[8:]], axis=0)


def _lru_kernel(xg_ref, cw_ref, cb_ref, wg_ref, bg_ref, lam_ref, gn_ref, y_ref, tail_ref, h_ref):
    t = pl.program_id(1)

    @pl.when(t == 0)
    def _():
        tail_ref[...] = jnp.zeros_like(tail_ref)
        h_ref[...] = jnp.zeros_like(h_ref)

    x = xg_ref[:, :LRU_WIDTH]
    gate = xg_ref[:, LRU_WIDTH:]
    tt = x.shape[0]
    tail = tail_ref[...]
    xc = cb_ref[...] + cw_ref[CONV_WIDTH - 1:CONV_WIDTH, :] * x
    for k in range(1, CONV_WIDTH):
        xs = _shift_rows(x, k, pltpu.roll(tail, k, axis=0))
        xc = xc + cw_ref[CONV_WIDTH - 1 - k:CONV_WIDTH - k, :] * xs
    tail_ref[...] = x[tt - 8:]

    gz = jnp.dot(xc.astype(BF16), wg_ref[...], preferred_element_type=F32) + bg_ref[...]
    r = _sigmoid(gz[:, :LRU_WIDTH])
    ig = _sigmoid(gz[:, LRU_WIDTH:])
    log_a = (-LRU_C) * r * _softplus(-lam_ref[...])
    a = jnp.exp(log_a)
    u = jnp.sqrt(jnp.tanh(-log_a) * (1.0 + a * a)) * (ig * xc)

    ones8 = jnp.ones((8, LRU_WIDTH), F32)
    zeros8 = jnp.zeros((8, LRU_WIDTH), F32)
    k = 1
    while k < tt:
        if k < 8:
            a_sh = _shift_rows(a, k, ones8)
            u_sh = _shift_rows(u, k, zeros8)
        else:
            a_sh = jnp.concatenate([jnp.ones((k, LRU_WIDTH), F32), a[:tt - k]], axis=0)
            u_sh = jnp.concatenate([jnp.zeros((k, LRU_WIDTH), F32), u[:tt - k]], axis=0)
        u = a * u_sh + u
        a = a * a_sh
        k *= 2
    h = u + a * h_ref[...]
    h_ref[...] = h[tt - 1:tt]

    y = h * jax.nn.gelu(gate, approximate=True)
    y_ref[...] = _rms(y, gn_ref[...]).astype(BF16)


def _blockdiag(w):
    g, i, j = w.shape
    eye = jnp.eye(g, dtype=w.dtype)
    return (eye[:, None, :, None] * w[:, :, None, :]).reshape(g * i, g * j)


def _lru_branch(xg, batch, seq, conv_w, conv_b, w_r, b_r, w_i, b_i, lam, g_lru, tt=256):
    n = xg.shape[0]
    nt = seq // tt
    wg = jnp.concatenate([_blockdiag(w_r), _blockdiag(w_i)], axis=1).astype(BF16)
    bg = jnp.concatenate([b_r.reshape(1, LRU_WIDTH), b_i.reshape(1, LRU_WIDTH)], axis=1)
    const = lambda shape: pl.BlockSpec(shape, lambda b, t: (0, 0))
    return pl.pallas_call(
        _lru_kernel,
        grid=(batch, nt),
        in_specs=[
            pl.BlockSpec((tt, 2 * LRU_WIDTH), lambda b, t: (b * nt + t, 0)),
            const((CONV_WIDTH, LRU_WIDTH)),
            const((1, LRU_WIDTH)),
            const((LRU_WIDTH, 2 * LRU_WIDTH)),
            const((1, 2 * LRU_WIDTH)),
            const((1, LRU_WIDTH)),
            const((1, LRU_WIDTH)),
        ],
        out_specs=pl.BlockSpec((tt, LRU_WIDTH), lambda b, t: (b * nt + t, 0)),
        out_shape=jax.ShapeDtypeStruct((n, LRU_WIDTH), BF16),
        scratch_shapes=[pltpu.VMEM((8, LRU_WIDTH), F32), pltpu.VMEM((1, LRU_WIDTH), F32)],
        compiler_params=_params("parallel", "arbitrary"),
        name="lru_branch",
    )(xg, conv_w, conv_b.reshape(1, LRU_WIDTH), wg, bg, lam.reshape(1, LRU_WIDTH),
      g_lru.reshape(1, LRU_WIDTH))


def _sb_block(qm, r_run, kb, vb, u_neg, mask):
    z = lax.dot_general(qm, kb, (((1,), (1,)), ((), ())), preferred_element_type=F32)
    sp = _softplus(z)
    if mask is not None:
        sp = jnp.where(mask, sp, 0.0)
    c = jnp.dot(sp.astype(BF16), u_neg, preferred_element_type=F32)
    reps = z.shape[1] // LANES
    w = jnp.exp(z + c + jnp.concatenate([r_run] * reps, axis=1))
    if mask is not None:
        w = jnp.where(mask, w, 0.0)
    inc = jnp.dot(w.astype(BF16), vb, preferred_element_type=F32)
    return inc, r_run + c[:, 0:1]


def _attn_kernel(q_ref, k_ref, v_ref, u_ref, o_ref, acc_ref, r_ref, *, tq, tk):
    i = pl.program_id(2)
    lane = lax.broadcasted_iota(jnp.int32, (1, LANES), 1)
    q = q_ref[...]
    u_neg = u_ref[...]
    nd = tq // tk
    outs = []
    for hh in range(2):
        hmask = (lane >= SB_HEAD_DIM * hh) & (lane < SB_HEAD_DIM * (hh + 1))
        qm = jnp.where(hmask, q, jnp.zeros_like(q))
        acc_ref[...] = jnp.zeros_like(acc_ref)
        r_ref[...] = jnp.zeros_like(r_ref)
        for d in reversed(range(nd)):
            r0 = d * tk
            m = tq - r0
            start = pl.multiple_of(i * tq + r0, tk)
            kb = k_ref[pl.ds(start, tk), :]
            vb = v_ref[pl.ds(start, tk), :]
            rowi = lax.broadcasted_iota(jnp.int32, (m, tk), 0)
            coli = lax.broadcasted_iota(jnp.int32, (m, tk), 1)
            mask = coli < rowi
            inc, rn = _sb_block(qm[r0:], r_ref[r0:, :], kb, vb, u_neg, mask)
            acc_ref[r0:, :] += inc
            r_ref[r0:, :] = rn

        def body(jj, carry):
            j = i * nd - 1 - jj
            start = pl.multiple_of(j * tk, tk)
            kb = k_ref[pl.ds(start, tk), :]
            vb = v_ref[pl.ds(start, tk), :]
            inc, rn = _sb_block(qm, r_ref[...], kb, vb, u_neg, None)
            acc_ref[...] += inc
            r_ref[...] = rn
            return carry

        lax.fori_loop(0, i * nd, body, 0)
        outs.append(acc_ref[...])
    o_ref[...] = jnp.where(lane < SB_HEAD_DIM, outs[0], outs[1])


def _sb_attention(qkv, batch, seq, tq=512, tk=256):
    n = qkv.shape[0]
    nq = seq // tq
    hp = SB_WIDTH // LANES
    row = lax.broadcasted_iota(jnp.int32, (tk, tk), 0)
    col = lax.broadcasted_iota(jnp.int32, (tk, tk), 1)
    u_neg = jnp.where(row >= col, -1.0, 0.0).astype(BF16)
    return pl.pallas_call(
        functools.partial(_attn_kernel, tq=tq, tk=tk),
        grid=(batch, hp, nq),
        in_specs=[
            pl.BlockSpec((tq, LANES), lambda b, h, i: (b * nq + i, h)),
            pl.BlockSpec((seq, LANES), lambda b, h, i: (b, hp + h)),
            pl.BlockSpec((seq, LANES), lambda b, h, i: (b, 2 * hp + h)),
            pl.BlockSpec((tk, tk), lambda b, h, i: (0, 0)),
        ],
        out_specs=pl.BlockSpec((tq, LANES), lambda b, h, i: (b * nq + i, h)),
        out_shape=jax.ShapeDtypeStruct((n, SB_WIDTH), F32),
        scratch_shapes=[pltpu.VMEM((tq, LANES), F32), pltpu.VMEM((tq, LANES), F32)],
        compiler_params=_params("parallel", "parallel", "arbitrary"),
        name="sb_attention",
    )(qkv, qkv, qkv, u_neg)


def _outproj_kernel(x_ref, yl_ref, ys_ref, gs_ref, wt_ref, wb_ref, gn_ref, *rest, route):
    if route:
        rwh_ref, rwl_ref, x1_ref, h1_ref, route_ref = rest
    else:
        x1_ref, h1_ref = rest
    ysn = _rms(ys_ref[...], gs_ref[...]).astype(BF16)
    x1 = (x_ref[...] + jnp.dot(yl_ref[...], wt_ref[...], preferred_element_type=F32)
          + jnp.dot(ysn, wb_ref[...], preferred_element_type=F32))
    x1_ref[...] = x1
    h = _rms(x1, gn_ref[...])
    h_hi = h.astype(BF16)
    h1_ref[...] = h_hi
    if route:
        h_lo = (h - h_hi.astype(F32)).astype(BF16)
        rwh = rwh_ref[...]
        logits = (jnp.dot(h_hi, rwh, preferred_element_type=F32)
                  + jnp.dot(h_lo, rwh, preferred_element_type=F32)
                  + jnp.dot(h_hi, rwl_ref[...], preferred_element_type=F32))
        lane = lax.broadcasted_iota(jnp.int32, logits.shape, 1).astype(F32)
        neg = jnp.float32(-jnp.inf)
        lg = jnp.where(lane < N_EXPERTS, logits, neg)
        m1 = jnp.max(lg, axis=1, keepdims=True)
        i1 = jnp.min(jnp.where(lg == m1, lane, float(LANES)), axis=1, keepdims=True)
        lg2 = jnp.where(lane == i1, neg, lg)
        m2 = jnp.max(lg2, axis=1, keepdims=True)
        i2 = jnp.min(jnp.where(lg2 == m2, lane, float(LANES)), axis=1, keepdims=True)
        e2 = jnp.exp(m2 - m1)
        g1 = 1.0 / (1.0 + e2)
        g2 = e2 / (1.0 + e2)
        route_ref[...] = jnp.where(lane == 0, i1, jnp.where(lane == 1, i2, jnp.where(
            lane == 2, g1, jnp.where(lane == 3, g2, 0.0))))


def _outproj(x, ylru, ysb, g_sb, w_out, g_ffn, router_w=None, tm=512):
    n = x.shape[0]
    route = router_w is not None
    wt = w_out[:LRU_WIDTH].astype(BF16)
    wb = w_out[LRU_WIDTH:].astype(BF16)
    const = lambda shape: pl.BlockSpec(shape, lambda i: (0, 0))
    in_specs = [
        pl.BlockSpec((tm, D_MODEL), lambda i: (i, 0)),
        pl.BlockSpec((tm, LRU_WIDTH), lambda i: (i, 0)),
        pl.BlockSpec((tm, SB_WIDTH), lambda i: (i, 0)),
        const((1, SB_WIDTH)),
        const((LRU_WIDTH, D_MODEL)),
        const((SB_WIDTH, D_MODEL)),
        const((1, D_MODEL)),
    ]
    args = [x, ylru, ysb, g_sb.reshape(1, SB_WIDTH), wt, wb, g_ffn.reshape(1, D_MODEL)]
    out_specs = [pl.BlockSpec((tm, D_MODEL), lambda i: (i, 0)), pl.BlockSpec((tm, D_MODEL), lambda i: (i, 0))]
    out_shape = [jax.ShapeDtypeStruct((n, D_MODEL), F32), jax.ShapeDtypeStruct((n, D_MODEL), BF16)]
    if route:
        rw = jnp.pad(router_w, ((0, 0), (0, LANES - N_EXPERTS)))
        rw_hi = rw.astype(BF16)
        rw_lo = (rw - rw_hi.astype(F32)).astype(BF16)
        in_specs += [const((D_MODEL, LANES)), const((D_MODEL, LANES))]
        args += [rw_hi, rw_lo]
        out_specs.append(pl.BlockSpec((tm, LANES), lambda i: (i, 0)))
        out_shape.append(jax.ShapeDtypeStruct((n, LANES), F32))
    return pl.pallas_call(
        functools.partial(_outproj_kernel, route=route),
        grid=(n // tm,),
        in_specs=in_specs,
        out_specs=out_specs,
        out_shape=out_shape,
        compiler_params=_params("parallel"),
        name="outproj_route" if route else "outproj",
    )(*args)


def _swiglu_chunk(h, wg, wu, wd):
    g = jnp.dot(h, wg, preferred_element_type=F32)
    u = jnp.dot(h, wu, preferred_element_type=F32)
    a = (g * _sigmoid(g)) * u
    return jnp.dot(a.astype(BF16), wd, preferred_element_type=F32)


def _ffn_kernel(h_ref, x_ref, wg_ref, wu_ref, wd_ref, o_ref, acc_ref):
    c = pl.program_id(1)
    part = _swiglu_chunk(h_ref[...], wg_ref[...], wu_ref[...], wd_ref[...])

    @pl.when(c == 0)
    def _():
        acc_ref[...] = x_ref[...] + part

    @pl.when(c > 0)
    def _():
        acc_ref[...] += part

    @pl.when(c == pl.num_programs(1) - 1)
    def _():
        o_ref[...] = acc_ref[...]


def _dense_ffn(h, x, w_g, w_u, w_d, tm=512, tf=1024):
    n = h.shape[0]
    return pl.pallas_call(
        _ffn_kernel,
        grid=(n // tm, D_FF // tf),
        in_specs=[
            pl.BlockSpec((tm, D_MODEL), lambda i, c: (i, 0)),
            pl.BlockSpec((tm, D_MODEL), lambda i, c: (i, 0)),
            pl.BlockSpec((D_MODEL, tf), lambda i, c: (0, c)),
            pl.BlockSpec((D_MODEL, tf), lambda i, c: (0, c)),
            pl.BlockSpec((tf, D_MODEL), lambda i, c: (c, 0)),
        ],
        out_specs=pl.BlockSpec((tm, D_MODEL), lambda i, c: (i, 0)),
        out_shape=jax.ShapeDtypeStruct((n, D_MODEL), F32),
        scratch_shapes=[pltpu.VMEM((tm, D_MODEL), F32)],
        compiler_params=_params("parallel", "arbitrary"),
        name="dense_ffn",
    )(h, x, w_g.astype(BF16), w_u.astype(BF16), w_d.astype(BF16))


def _expert_kernel(be_ref, x_ref, wg_ref, wu_ref, wd_ref, o_ref, acc_ref):
    c = pl.program_id(1)
    part = _swiglu_chunk(x_ref[...], wg_ref[0], wu_ref[0], wd_ref[0])

    @pl.when(c == 0)
    def _():
        acc_ref[...] = part

    @pl.when(c > 0)
    def _():
        acc_ref[...] += part

    @pl.when(c == pl.num_programs(1) - 1)
    def _():
        o_ref[...] = acc_ref[...]


def _expert_ffn(x_rows, block_expert, w_g, w_u, w_d, blk, tf=1024):
    n_rows = x_rows.shape[0]
    grid_spec = pltpu.PrefetchScalarGridSpec(
        num_scalar_prefetch=1,
        grid=(n_rows // blk, D_FF // tf),
        in_specs=[
            pl.BlockSpec((blk, D_MODEL), lambda i, c, be: (i, 0)),
            pl.BlockSpec((1, D_MODEL, tf), lambda i, c, be: (be[i], 0, c)),
            pl.BlockSpec((1, D_MODEL, tf), lambda i, c, be: (be[i], 0, c)),
            pl.BlockSpec((1, tf, D_MODEL), lambda i, c, be: (be[i], c, 0)),
        ],
        out_specs=pl.BlockSpec((blk, D_MODEL), lambda i, c, be: (i, 0)),
        scratch_shapes=[pltpu.VMEM((blk, D_MODEL), F32)],
    )
    return pl.pallas_call(
        _expert_kernel,
        grid_spec=grid_spec,
        out_shape=jax.ShapeDtypeStruct((n_rows, D_MODEL), F32),
        compiler_params=_params("parallel", "arbitrary"),
        name="expert_ffn",
    )(block_expert, x_rows, w_g.astype(BF16), w_u.astype(BF16), w_d.astype(BF16))


def _moe(h, route, w_g, w_u, w_d, blk=512):
    n = h.shape[0]
    n_assign = n * TOP_K
    flat_e = route[:, :TOP_K].astype(jnp.int32).reshape(-1)
    flat_gate = route[:, TOP_K:2 * TOP_K].reshape(-1)
    flat_tok = jnp.repeat(jnp.arange(n, dtype=jnp.int32), TOP_K)
    order = jnp.argsort(flat_e)
    e_sorted = flat_e[order]
    counts = jnp.bincount(flat_e, length=N_EXPERTS)
    padded = (counts + blk - 1) // blk * blk
    pad_end = jnp.cumsum(padded)
    pad_start = pad_end - padded
    start = jnp.cumsum(counts) - counts
    dest = pad_start[e_sorted] + jnp.arange(n_assign) - start[e_sorted]
    n_blocks = -(-n_assign // blk) + N_EXPERTS
    n_rows = n_blocks * blk
    row_tok = jnp.zeros((n_rows,), jnp.int32).at[dest].set(flat_tok[order])
    row_gate = jnp.zeros((n_rows,), F32).at[dest].set(flat_gate[order])
    block_start = jnp.arange(n_blocks) * blk
    block_expert = jnp.minimum(jnp.searchsorted(pad_end, block_start, side='right'),
                               N_EXPERTS - 1).astype(jnp.int32)
    x_rows = h[row_tok]
    y_rows = _expert_ffn(x_rows, block_expert, w_g, w_u, w_d, blk)
    return jnp.zeros((n, D_MODEL), F32).at[row_tok].add(y_rows * row_gate[:, None])


def _add_norm_kernel(x_ref, y_ref, g_ref, o_ref):
    o_ref[...] = _rms(x_ref[...] + y_ref[...], g_ref[...])


def _add_norm(x, y, g, tm=1024):
    n = x.shape[0]
    spec = pl.BlockSpec((tm, D_MODEL), lambda i: (i, 0))
    return pl.pallas_call(
        _add_norm_kernel,
        grid=(n // tm,),
        in_specs=[spec, spec, pl.BlockSpec((1, D_MODEL), lambda i: (0, 0))],
        out_specs=spec,
        out_shape=jax.ShapeDtypeStruct((n, D_MODEL), F32),
        compiler_params=_params("parallel"),
        name="add_norm",
    )(x, y, g.reshape(1, D_MODEL))


def kernel(x, mix_norm, w_in, conv_w, conv_b, w_rgate, b_rgate, w_igate, b_igate, lru_lambda, lru_out_norm, sb_out_norm, w_out, ffn_norm, dense_w_gate, dense_w_up, dense_w_down, router_w, moe_w_gate, moe_w_up, moe_w_down, final_norm):
    batch, seq, d = x.shape
    depth = mix_norm.shape[0]
    xf = x.reshape(batch * seq, d)
    moe_out = None
    for layer in range(depth):
        j = layer // 2
        is_moe = layer % 2 == 1
        xg, qkv = _norm_inproj(xf, mix_norm[layer], w_in[layer])
        ylru = _lru_branch(xg, batch, seq, conv_w[layer], conv_b[layer], w_rgate[layer], b_rgate[layer],
                           w_igate[layer], b_igate[layer], lru_lambda[layer], lru_out_norm[layer])
        ysb = _sb_attention(qkv, batch, seq)
        res = _outproj(xf, ylru, ysb, sb_out_norm[layer], w_out[layer], ffn_norm[layer],
                       router_w[j] if is_moe else None)
        if is_moe:
            x1, h1, route = res
            y = _moe(h1, route, moe_w_gate[j], moe_w_up[j], moe_w_down[j])
            if layer == depth - 1:
                xf, moe_out = x1, y
            else:
                xf = x1 + y
        else:
            x1, h1 = res
            xf = _dense_ffn(h1, x1, dense_w_gate[j], dense_w_up[j], dense_w_down[j])
    if moe_out is None:
        moe_out = jnp.zeros_like(xf)
    out = _add_norm(xf, moe_out, final_norm)
    return out.reshape(batch, seq, d)
```

```python
import functools

import jax
import jax.numpy as jnp
from jax import lax
from jax.experimental import pallas as pl
from jax.experimental.pallas import tpu as pltpu

D_MODEL = 1024
LRU_WIDTH = 512
LRU_BLOCKS = 8
LRU_BLOCK_DIM = LRU_WIDTH // LRU_BLOCKS
CONV_WIDTH = 4
LRU_C = 8.0
SB_HEADS = 8
SB_HEAD_DIM = 64
SB_WIDTH = SB_HEADS * SB_HEAD_DIM
D_FF = 3072
N_EXPERTS = 8
TOP_K = 2
NORM_EPS = 1e-6

LOG2E = 1.4426950408889634
EXIT_LOG2 = -152.0
DEAD_LOG2 = -1e30

LANES = 128
VMEM_LIMIT = 56 * 1024 * 1024

BF16 = jnp.bfloat16
F32 = jnp.float32


def _params(*sem):
    return pltpu.CompilerParams(dimension_semantics=sem, vmem_limit_bytes=VMEM_LIMIT)


def _rms(x, g):
    return x * lax.rsqrt(jnp.mean(x * x, axis=-1, keepdims=True) + NORM_EPS) * g


def _sigmoid(x):
    return 1.0 / (1.0 + jnp.exp(-x))


def _softplus(x):
    return jnp.maximum(x, 0.0) + jnp.log(1.0 + jnp.exp(-jnp.abs(x)))


def _norm_inproj_kernel(x_ref, g_ref, wa_ref, wb_ref, xg_ref, qkv_ref):
    h = _rms(x_ref[...], g_ref[...]).astype(BF16)
    xg_ref[...] = jnp.dot(h, wa_ref[...], preferred_element_type=F32)
    p = jnp.dot(h, wb_ref[...], preferred_element_type=F32)
    scale = SB_HEAD_DIM ** -0.5 * LOG2E
    qkv_ref[:, :SB_WIDTH] = (p[:, :SB_WIDTH] * scale).astype(BF16)
    qkv_ref[:, SB_WIDTH:] = p[:, SB_WIDTH:].astype(BF16)


def _norm_inproj(x, g, w_in, tm=512):
    n = x.shape[0]
    wa = w_in[:, :2 * LRU_WIDTH].astype(BF16)
    wb = w_in[:, 2 * LRU_WIDTH:].astype(BF16)
    return pl.pallas_call(
        _norm_inproj_kernel,
        grid=(n // tm,),
        in_specs=[
            pl.BlockSpec((tm, D_MODEL), lambda i: (i, 0)),
            pl.BlockSpec((1, D_MODEL), lambda i: (0, 0)),
            pl.BlockSpec((D_MODEL, 2 * LRU_WIDTH), lambda i: (0, 0)),
            pl.BlockSpec((D_MODEL, 3 * SB_WIDTH), lambda i: (0, 0)),
        ],
        out_specs=[
            pl.BlockSpec((tm, 2 * LRU_WIDTH), lambda i: (i, 0)),
            pl.BlockSpec((tm, 3 * SB_WIDTH), lambda i: (i, 0)),
        ],
        out_shape=[
            jax.ShapeDtypeStruct((n, 2 * LRU_WIDTH), F32),
            jax.ShapeDtypeStruct((n, 3 * SB_WIDTH), BF16),
        ],
        compiler_params=_params("parallel"),
        name="norm_inproj",
    )(x, g.reshape(1, D_MODEL), wa, wb)


def _shift_rows(x, k, fill_first8):
    xs = pltpu.roll(x, k, axis=0)
    row8 = lax.broadcasted_iota(jnp.int32, (8, x.shape[1]), 0)
    first = jnp.where(row8 < k, fill_first8, xs[:8])
    return jnp.concatenate([first, xs[8:]], axis=0)


def _lru_kernel(xg_ref, cw_ref, cb_ref, wg_ref, bg_ref, lam_ref, gn_ref, y_ref, tail_ref, h_ref):
    t = pl.program_id(1)

    @pl.when(t == 0)
    def _():
        tail_ref[...] = jnp.zeros_like(tail_ref)
        h_ref[...] = jnp.zeros_like(h_ref)

    x = xg_ref[:, :LRU_WIDTH]
    gate = xg_ref[:, LRU_WIDTH:]
    tt = x.shape[0]
    tail = tail_ref[...]
    xc = cb_ref[...] + cw_ref[CONV_WIDTH - 1:CONV_WIDTH, :] * x
    for k in range(1, CONV_WIDTH):
        xs = _shift_rows(x, k, pltpu.roll(tail, k, axis=0))
        xc = xc + cw_ref[CONV_WIDTH - 1 - k:CONV_WIDTH - k, :] * xs
    tail_ref[...] = x[tt - 8:]

    gz = jnp.dot(xc.astype(BF16), wg_ref[...], preferred_element_type=F32) + bg_ref[...]
    r = _sigmoid(gz[:, :LRU_WIDTH])
    ig = _sigmoid(gz[:, LRU_WIDTH:])
    log_a = (-LRU_C) * r * _softplus(-lam_ref[...])
    a = jnp.exp(log_a)
    u = jnp.sqrt(jnp.tanh(-log_a) * (1.0 + a * a)) * (ig * xc)

    ones8 = jnp.ones((8, LRU_WIDTH), F32)
    zeros8 = jnp.zeros((8, LRU_WIDTH), F32)
    k = 1
    while k < tt:
        if k < 8:
            a_sh = _shift_rows(a, k, ones8)
            u_sh = _shift_rows(u, k, zeros8)
        else:
            a_sh = jnp.concatenate([jnp.ones((k, LRU_WIDTH), F32), a[:tt - k]], axis=0)
            u_sh = jnp.concatenate([jnp.zeros((k, LRU_WIDTH), F32), u[:tt - k]], axis=0)
        u = a * u_sh + u
        a = a * a_sh
        k *= 2
    h = u + a * h_ref[...]
    h_ref[...] = h[tt - 1:tt]

    y = h * jax.nn.gelu(gate, approximate=True)
    y_ref[...] = _rms(y, gn_ref[...]).astype(BF16)


def _blockdiag(w):
    g, i, j = w.shape
    eye = jnp.eye(g, dtype=w.dtype)
    return (eye[:, None, :, None] * w[:, :, None, :]).reshape(g * i, g * j)


def _lru_branch(xg, batch, seq, conv_w, conv_b, w_r, b_r, w_i, b_i, lam, g_lru, tt=256):
    n = xg.shape[0]
    nt = seq // tt
    wg = jnp.concatenate([_blockdiag(w_r), _blockdiag(w_i)], axis=1).astype(BF16)
    bg = jnp.concatenate([b_r.reshape(1, LRU_WIDTH), b_i.reshape(1, LRU_WIDTH)], axis=1)
    const = lambda shape: pl.BlockSpec(shape, lambda b, t: (0, 0))
    return pl.pallas_call(
        _lru_kernel,
        grid=(batch, nt),
        in_specs=[
            pl.BlockSpec((tt, 2 * LRU_WIDTH), lambda b, t: (b * nt + t, 0)),
            const((CONV_WIDTH, LRU_WIDTH)),
            const((1, LRU_WIDTH)),
            const((LRU_WIDTH, 2 * LRU_WIDTH)),
            const((1, 2 * LRU_WIDTH)),
            const((1, LRU_WIDTH)),
            const((1, LRU_WIDTH)),
        ],
        out_specs=pl.BlockSpec((tt, LRU_WIDTH), lambda b, t: (b * nt + t, 0)),
        out_shape=jax.ShapeDtypeStruct((n, LRU_WIDTH), BF16),
        scratch_shapes=[pltpu.VMEM((8, LRU_WIDTH), F32), pltpu.VMEM((1, LRU_WIDTH), F32)],
        compiler_params=_params("parallel", "arbitrary"),
        name="lru_branch",
    )(xg, conv_w, conv_b.reshape(1, LRU_WIDTH), wg, bg, lam.reshape(1, LRU_WIDTH),
      g_lru.reshape(1, LRU_WIDTH))


def _sb_scores(qm, kb):
    return lax.dot_general(qm, kb, (((1,), (1,)), ((), ())), preferred_element_type=F32)


def _sb_cumsum(z, u_neg, mask):
    neg_abs = lax.bitcast_convert_type(lax.bitcast_convert_type(z, jnp.uint32) | jnp.uint32(0x80000000), F32)
    sp = jnp.maximum(z, 0.0) + jnp.log(1.0 + jnp.exp2(neg_abs)) * LOG2E
    if mask is not None:
        sp = jnp.where(mask, sp, 0.0)
    return jnp.dot(sp.astype(BF16), u_neg, preferred_element_type=F32)


def _sb_apply(z, c, r_run, vb, mask):
    arg = z + c
    if r_run is not None:
        arg = arg + jnp.concatenate([r_run] * (z.shape[1] // LANES), axis=1)
    w = jnp.exp2(arg)
    if mask is not None:
        w = jnp.where(mask, w, 0.0)
    inc = jnp.dot(w.astype(BF16), vb, preferred_element_type=F32)
    total = jnp.broadcast_to(c[:, 0:1], inc.shape)
    return inc, total if r_run is None else r_run + total


def _attn_kernel(q_ref, k_ref, v_ref, u_ref, o_ref, acc_ref, r_ref, *, t, gq):
    i = pl.program_id(2)
    lane = lax.broadcasted_iota(jnp.int32, (1, LANES), 1)
    q = q_ref[...]
    u_neg = u_ref[...]
    qm = []
    for hh in range(2):
        hmask = (lane >= SB_HEAD_DIM * hh) & (lane < SB_HEAD_DIM * (hh + 1))
        qm.append(jnp.where(hmask, q, jnp.zeros_like(q)))
    rowi = lax.broadcasted_iota(jnp.int32, (t, t), 0)
    coli = lax.broadcasted_iota(jnp.int32, (t, t), 1)
    tri = coli < rowi
    ids = [(g, hh) for g in range(gq) for hh in range(2)]
    rows = [slice(g * t, (g + 1) * t) for g in range(gq)]
    qs = [qm[hh][rows[g]] for g, hh in ids]

    def load_kv(kidx):
        start = pl.multiple_of(jnp.maximum(kidx, 0) * t, t)
        return k_ref[pl.ds(start, t), :], v_ref[pl.ds(start, t), :]

    def dead_if_past_start(kidx, r):
        return jnp.where(kidx >= 0, r, DEAD_LOG2)

    kv0 = [load_kv(i * gq + g) for g in range(gq)]
    kv1 = [load_kv(i * gq + g - 1) for g in range(gq)]
    z0 = [_sb_scores(qc, kv0[g][0]) for qc, (g, _) in zip(qs, ids)]
    z1 = [_sb_scores(qc, kv1[g][0]) for qc, (g, _) in zip(qs, ids)]
    c0 = [_sb_cumsum(z, u_neg, tri) for z in z0]
    c1 = [_sb_cumsum(z, u_neg, None) for z in z1]
    out0 = [_sb_apply(z, c, None, kv0[g][1], tri) for z, c, (g, _) in zip(z0, c0, ids)]
    rmax = None
    for z, c, (inc0, r0), (g, hh) in zip(z1, c1, out0, ids):
        inc1, r1 = _sb_apply(z, c, dead_if_past_start(i * gq + g - 1, r0), kv1[g][1], None)
        acc_ref[hh, rows[g], :] = inc0 + inc1
        r_ref[hh, rows[g], :] = r1
        rmax = r1 if rmax is None else jnp.maximum(rmax, r1)

    last_step = i * gq + gq - 1

    def is_live(rmax):
        return (jnp.max(rmax) > EXIT_LOG2).astype(jnp.int32)

    def cond(carry):
        s, live = carry
        return jnp.logical_and(s <= last_step, live > 0)

    def body(carry):
        s, _ = carry
        kv = [load_kv(i * gq + g - s) for g in range(gq)]
        zs = [_sb_scores(qc, kv[g][0]) for qc, (g, _) in zip(qs, ids)]
        cs = [_sb_cumsum(z, u_neg, None) for z in zs]
        rmax = None
        for z, c, (g, hh) in zip(zs, cs, ids):
            r_in = dead_if_past_start(i * gq + g - s, r_ref[hh, rows[g], :])
            inc, rn = _sb_apply(z, c, r_in, kv[g][1], None)
            acc_ref[hh, rows[g], :] += inc
            r_ref[hh, rows[g], :] = rn
            rmax = rn if rmax is None else jnp.maximum(rmax, rn)
        return s + 1, is_live(rmax)

    lax.while_loop(cond, body, (jnp.int32(2), is_live(rmax)))
    o_ref[...] = jnp.where(lane < SB_HEAD_DIM, acc_ref[0], acc_ref[1])


def _sb_attention(qkv, batch, seq, t=256, gq=2):
    n = qkv.shape[0]
    tq = t * gq
    nq = seq // tq
    hp = SB_WIDTH // LANES
    row = lax.broadcasted_iota(jnp.int32, (t, t), 0)
    col = lax.broadcasted_iota(jnp.int32, (t, t), 1)
    u_neg = jnp.where(row >= col, -1.0, 0.0).astype(BF16)
    return pl.pallas_call(
        functools.partial(_attn_kernel, t=t, gq=gq),
        grid=(batch, hp, nq),
        in_specs=[
            pl.BlockSpec((tq, LANES), lambda b, h, i: (b * nq + i, h)),
            pl.BlockSpec((seq, LANES), lambda b, h, i: (b, hp + h)),
            pl.BlockSpec((seq, LANES), lambda b, h, i: (b, 2 * hp + h)),
            pl.BlockSpec((t, t), lambda b, h, i: (0, 0)),
        ],
        out_specs=pl.BlockSpec((tq, LANES), lambda b, h, i: (b * nq + i, h)),
        out_shape=jax.ShapeDtypeStruct((n, SB_WIDTH), F32),
        scratch_shapes=[pltpu.VMEM((2, tq, LANES), F32), pltpu.VMEM((2, tq, LANES), F32)],
        compiler_params=_params("parallel", "parallel", "arbitrary"),
        name="sb_attention",
    )(qkv, qkv, qkv, u_neg)


def _outproj_kernel(x_ref, yl_ref, ys_ref, gs_ref, wt_ref, wb_ref, gn_ref, *rest, route):
    if route:
        rwh_ref, rwl_ref, x1_ref, h1_ref, route_ref = rest
    else:
        x1_ref, h1_ref = rest
    ysn = _rms(ys_ref[...], gs_ref[...]).astype(BF16)
    x1 = (x_ref[...] + jnp.dot(yl_ref[...], wt_ref[...], preferred_element_type=F32)
          + jnp.dot(ysn, wb_ref[...], preferred_element_type=F32))
    x1_ref[...] = x1
    h = _rms(x1, gn_ref[...])
    h_hi = h.astype(BF16)
    h1_ref[...] = h_hi
    if route:
        h_lo = (h - h_hi.astype(F32)).astype(BF16)
        rwh = rwh_ref[...]
        logits = (jnp.dot(h_hi, rwh, preferred_element_type=F32)
                  + jnp.dot(h_lo, rwh, preferred_element_type=F32)
                  + jnp.dot(h_hi, rwl_ref[...], preferred_element_type=F32))
        lane = lax.broadcasted_iota(jnp.int32, logits.shape, 1).astype(F32)
        neg = jnp.float32(-jnp.inf)
        lg = jnp.where(lane < N_EXPERTS, logits, neg)
        m1 = jnp.max(lg, axis=1, keepdims=True)
        i1 = jnp.min(jnp.where(lg == m1, lane, float(LANES)), axis=1, keepdims=True)
        lg2 = jnp.where(lane == i1, neg, lg)
        m2 = jnp.max(lg2, axis=1, keepdims=True)
        i2 = jnp.min(jnp.where(lg2 == m2, lane, float(LANES)), axis=1, keepdims=True)
        e2 = jnp.exp(m2 - m1)
        g1 = 1.0 / (1.0 + e2)
        g2 = e2 / (1.0 + e2)
        route_ref[...] = jnp.where(lane == 0, i1, jnp.where(lane == 1, i2, jnp.where(
            lane == 2, g1, jnp.where(lane == 3, g2, 0.0))))


def _outproj(x, ylru, ysb, g_sb, w_out, g_ffn, router_w=None, tm=512):
    n = x.shape[0]
    route = router_w is not None
    wt = w_out[:LRU_WIDTH].astype(BF16)
    wb = w_out[LRU_WIDTH:].astype(BF16)
    const = lambda shape: pl.BlockSpec(shape, lambda i: (0, 0))
    in_specs = [
        pl.BlockSpec((tm, D_MODEL), lambda i: (i, 0)),
        pl.BlockSpec((tm, LRU_WIDTH), lambda i: (i, 0)),
        pl.BlockSpec((tm, SB_WIDTH), lambda i: (i, 0)),
        const((1, SB_WIDTH)),
        const((LRU_WIDTH, D_MODEL)),
        const((SB_WIDTH, D_MODEL)),
        const((1, D_MODEL)),
    ]
    args = [x, ylru, ysb, g_sb.reshape(1, SB_WIDTH), wt, wb, g_ffn.reshape(1, D_MODEL)]
    out_specs = [pl.BlockSpec((tm, D_MODEL), lambda i: (i, 0)), pl.BlockSpec((tm, D_MODEL), lambda i: (i, 0))]
    out_shape = [jax.ShapeDtypeStruct((n, D_MODEL), F32), jax.ShapeDtypeStruct((n, D_MODEL), BF16)]
    if route:
        rw = jnp.pad(router_w, ((0, 0), (0, LANES - N_EXPERTS)))
        rw_hi = rw.astype(BF16)
        rw_lo = (rw - rw_hi.astype(F32)).astype(BF16)
        in_specs += [const((D_MODEL, LANES)), const((D_MODEL, LANES))]
        args += [rw_hi, rw_lo]
        out_specs.append(pl.BlockSpec((tm, LANES), lambda i: (i, 0)))
        out_shape.append(jax.ShapeDtypeStruct((n, LANES), F32))
    return pl.pallas_call(
        functools.partial(_outproj_kernel, route=route),
        grid=(n // tm,),
        in_specs=in_specs,
        out_specs=out_specs,
        out_shape=out_shape,
        compiler_params=_params("parallel"),
        name="outproj_route" if route else "outproj",
    )(*args)


def _swiglu_chunk(h, wg, wu, wd):
    g = jnp.dot(h, wg, preferred_element_type=F32)
    u = jnp.dot(h, wu, preferred_element_type=F32)
    a = (g * _sigmoid(g)) * u
    return jnp.dot(a.astype(BF16), wd, preferred_element_type=F32)


def _ffn_kernel(h_ref, x_ref, wg_ref, wu_ref, wd_ref, o_ref, acc_ref):
    c = pl.program_id(1)
    part = _swiglu_chunk(h_ref[...], wg_ref[...], wu_ref[...], wd_ref[...])

    @pl.when(c == 0)
    def _():
        acc_ref[...] = x_ref[...] + part

    @pl.when(c > 0)
    def _():
        acc_ref[...] += part

    @pl.when(c == pl.num_programs(1) - 1)
    def _():
        o_ref[...] = acc_ref[...]


def _dense_ffn(h, x, w_g, w_u, w_d, tm=512, tf=1024):
    n = h.shape[0]
    return pl.pallas_call(
        _ffn_kernel,
        grid=(n // tm, D_FF // tf),
        in_specs=[
            pl.BlockSpec((tm, D_MODEL), lambda i, c: (i, 0)),
            pl.BlockSpec((tm, D_MODEL), lambda i, c: (i, 0)),
            pl.BlockSpec((D_MODEL, tf), lambda i, c: (0, c)),
            pl.BlockSpec((D_MODEL, tf), lambda i, c: (0, c)),
            pl.BlockSpec((tf, D_MODEL), lambda i, c: (c, 0)),
        ],
        out_specs=pl.BlockSpec((tm, D_MODEL), lambda i, c: (i, 0)),
        out_shape=jax.ShapeDtypeStruct((n, D_MODEL), F32),
        scratch_shapes=[pltpu.VMEM((tm, D_MODEL), F32)],
        compiler_params=_params("parallel", "arbitrary"),
        name="dense_ffn",
    )(h, x, w_g.astype(BF16), w_u.astype(BF16), w_d.astype(BF16))


def _expert_kernel(be_ref, x_ref, wg_ref, wu_ref, wd_ref, o_ref, acc_ref):
    c = pl.program_id(1)
    part = _swiglu_chunk(x_ref[...], wg_ref[0], wu_ref[0], wd_ref[0])

    @pl.when(c == 0)
    def _():
        acc_ref[...] = part

    @pl.when(c > 0)
    def _():
        acc_ref[...] += part

    @pl.when(c == pl.num_programs(1) - 1)
    def _():
        o_ref[...] = acc_ref[...]


def _expert_ffn(x_rows, block_expert, w_g, w_u, w_d, blk, tf=1024):
    n_rows = x_rows.shape[0]
    grid_spec = pltpu.PrefetchScalarGridSpec(
        num_scalar_prefetch=1,
        grid=(n_rows // blk, D_FF // tf),
        in_specs=[
            pl.BlockSpec((blk, D_MODEL), lambda i, c, be: (i, 0)),
            pl.BlockSpec((1, D_MODEL, tf), lambda i, c, be: (be[i], 0, c)),
            pl.BlockSpec((1, D_MODEL, tf), lambda i, c, be: (be[i], 0, c)),
            pl.BlockSpec((1, tf, D_MODEL), lambda i, c, be: (be[i], c, 0)),
        ],
        out_specs=pl.BlockSpec((blk, D_MODEL), lambda i, c, be: (i, 0)),
        scratch_shapes=[pltpu.VMEM((blk, D_MODEL), F32)],
    )
    return pl.pallas_call(
        _expert_kernel,
        grid_spec=grid_spec,
        out_shape=jax.ShapeDtypeStruct((n_rows, D_MODEL), F32),
        compiler_params=_params("parallel", "arbitrary"),
        name="expert_ffn",
    )(block_expert, x_rows, w_g.astype(BF16), w_u.astype(BF16), w_d.astype(BF16))


def _moe(h, route, w_g, w_u, w_d, blk=512):
    n = h.shape[0]
    n_assign = n * TOP_K
    flat_e = route[:, :TOP_K].astype(jnp.int32).reshape(-1)
    flat_gate = route[:, TOP_K:2 * TOP_K].reshape(-1)
    flat_tok = jnp.repeat(jnp.arange(n, dtype=jnp.int32), TOP_K)
    order = jnp.argsort(flat_e)
    e_sorted = flat_e[order]
    counts = jnp.bincount(flat_e, length=N_EXPERTS)
    padded = (counts + blk - 1) // blk * blk
    pad_end = jnp.cumsum(padded)
    pad_start = pad_end - padded
    start = jnp.cumsum(counts) - counts
    dest = pad_start[e_sorted] + jnp.arange(n_assign) - start[e_sorted]
    n_blocks = -(-n_assign // blk) + N_EXPERTS
    n_rows = n_blocks * blk
    row_tok = jnp.zeros((n_rows,), jnp.int32).at[dest].set(flat_tok[order])
    row_gate = jnp.zeros((n_rows,), F32).at[dest].set(flat_gate[order])
    block_start = jnp.arange(n_blocks) * blk
    block_expert = jnp.minimum(jnp.searchsorted(pad_end, block_start, side='right'),
                               N_EXPERTS - 1).astype(jnp.int32)
    x_rows = h[row_tok]
    y_rows = _expert_ffn(x_rows, block_expert, w_g, w_u, w_d, blk)
    return jnp.zeros((n, D_MODEL), F32).at[row_tok].add(y_rows * row_gate[:, None])


def _add_norm_kernel(x_ref, y_ref, g_ref, o_ref):
    o_ref[...] = _rms(x_ref[...] + y_ref[...], g_ref[...])


def _add_norm(x, y, g, tm=1024):
    n = x.shape[0]
    spec = pl.BlockSpec((tm, D_MODEL), lambda i: (i, 0))
    return pl.pallas_call(
        _add_norm_kernel,
        grid=(n // tm,),
        in_specs=[spec, spec, pl.BlockSpec((1, D_MODEL), lambda i: (0, 0))],
        out_specs=spec,
        out_shape=jax.ShapeDtypeStruct((n, D_MODEL), F32),
        compiler_params=_params("parallel"),
        name="add_norm",
    )(x, y, g.reshape(1, D_MODEL))


def kernel(x, mix_norm, w_in, conv_w, conv_b, w_rgate, b_rgate, w_igate, b_igate, lru_lambda, lru_out_norm, sb_out_norm, w_out, ffn_norm, dense_w_gate, dense_w_up, dense_w_down, router_w, moe_w_gate, moe_w_up, moe_w_down, final_norm):
    batch, seq, d = x.shape
    depth = mix_norm.shape[0]
    xf = x.reshape(batch * seq, d)
    moe_out = None
    for layer in range(depth):
        j = layer // 2
        is_moe = layer % 2 == 1
        xg, qkv = _norm_inproj(xf, mix_norm[layer], w_in[layer])
        ylru = _lru_branch(xg, batch, seq, conv_w[layer], conv_b[layer], w_rgate[layer], b_rgate[layer],
                           w_igate[layer], b_igate[layer], lru_lambda[layer], lru_out_norm[layer])
        ysb = _sb_attention(qkv, batch, seq)
        res = _outproj(xf, ylru, ysb, sb_out_norm[layer], w_out[layer], ffn_norm[layer],
                       router_w[j] if is_moe else None)
        if is_moe:
            x1, h1, route = res
            y = _moe(h1, route, moe_w_gate[j], moe_w_up[j], moe_w_down[j])
            if layer == depth - 1:
                xf, moe_out = x1, y
            else:
                xf = x1 + y
        else:
            x1, h1 = res
            xf = _dense_ffn(h1, x1, dense_w_gate[j], dense_w_up[j], dense_w_down[j])
    if moe_out is None:
        moe_out = jnp.zeros_like(xf)
    out = _add_norm(xf, moe_out, final_norm)
    return out.reshape(batch, seq, d)
```
